```python
import math
import jax, jax.numpy as jnp
from jax import lax
import numpy as np

D_MODEL = 1024
BATCH = 8
SEQ = 2048
DEPTH = 1
DEC_BATCH = 128
DEC_SEQ = 1
PAST_LEN = 16384
PAGE_SIZE = 128

D_MIX = D_MODEL
CONV_CH = D_MIX // 2
GDN_HEADS = 4
GDN_HEAD_DIM = (D_MIX - CONV_CH) // GDN_HEADS
GDN_DIM = GDN_HEADS * GDN_HEAD_DIM
CONV_A_K = 3
CONV_QKV_K = 4
CHUNK = 64
N_META = 16
N_GROUPS = 4
EXPERTS_PER_GROUP = 8
N_EXPERTS = N_GROUPS * EXPERTS_PER_GROUP
TOP_K = 2
D_FF = 256
EPS = 1e-6
IN_SIZES = (CONV_CH, CONV_CH, CONV_CH, GDN_DIM, GDN_DIM, GDN_DIM, GDN_DIM, GDN_HEADS, GDN_HEADS)
N_IN = sum(IN_SIZES)

kernel_name = "hymba_conv_gdn_hmoe_step"


def _rmsnorm(x, w):
    x32 = x.astype(jnp.float32)
    y = x32 * lax.rsqrt(jnp.mean(x32 * x32, axis=-1, keepdims=True) + EPS) * w.astype(jnp.float32)
    return y.astype(x.dtype)


def _l2norm(x):
    return x * lax.rsqrt(jnp.sum(x * x, axis=-1, keepdims=True) + EPS)


def _causal_dwconv(x, buf, w):
    width = w.shape[0]
    length = x.shape[1]
    xp = jnp.concatenate([buf.astype(x.dtype), x], axis=1)
    y = xp[:, 0:length] * w[0]
    for i in range(1, width):
        y = y + xp[:, i:i + length] * w[i]
    return y, xp[:, xp.shape[1] - (width - 1):]


def _gdn_chunks(q, k, v, g, beta, s0, chunk):
    b, length, h, dk = q.shape
    dv = v.shape[-1]
    n = length // chunk

    def to_chunks(t):
        t = t.reshape((b, n, chunk, h) + t.shape[3:])
        return jnp.moveaxis(jnp.moveaxis(t, 1, 0), 3, 2)

    idx = jnp.arange(chunk)
    causal = idx[:, None] >= idx[None, :]
    strict = idx[:, None] > idx[None, :]
    eye = jnp.eye(chunk, dtype=jnp.float32)

    def step(s, inp):
        qc, kc, vc, gc, bc = inp
        gcum = jnp.cumsum(gc, axis=-1)
        decay = jnp.exp(jnp.where(causal, gcum[..., :, None] - gcum[..., None, :], -jnp.inf))
        kb = kc * bc[..., None]
        lmat = jnp.where(strict, jnp.einsum('bhid,bhjd->bhij', kb, kc) * decay, 0.0)
        rhs = jnp.concatenate([vc * bc[..., None], kb * jnp.exp(gcum)[..., None]], axis=-1)
        sol = lax.linalg.triangular_solve(lmat + eye, rhs, left_side=True, lower=True,
                                          unit_diagonal=True)
        u = sol[..., :dv] - jnp.einsum('bhck,bhkv->bhcv', sol[..., dv:], s)
        attn = jnp.where(causal, jnp.einsum('bhik,bhjk->bhij', qc, kc) * decay, 0.0)
        o = (jnp.einsum('bhck,bhkv->bhcv', qc * jnp.exp(gcum)[..., None], s)
             + jnp.einsum('bhij,bhjv->bhiv', attn, u))
        g_last = gcum[..., -1:]
        s_new = (s * jnp.exp(g_last)[..., None]
                 + jnp.einsum('bhck,bhcv->bhkv', kc * jnp.exp(g_last - gcum)[..., None], u))
        return s_new, o

    s, o = lax.scan(step, s0, (to_chunks(q), to_chunks(k), to_chunks(v), to_chunks(g), to_chunks(beta)))
    o = jnp.moveaxis(jnp.moveaxis(o, 2, 3), 0, 1).reshape(b, length, h, dv)
    return o, s


def _moe(h, rg_w, rg_b, re_w, re_b, w_gate, w_up, w_down):
    shp = h.shape
    h2 = h.reshape(-1, shp[-1])
    t = h2.shape[0]
    lg = (h2 @ rg_w).astype(jnp.float32) + rg_b.astype(jnp.float32)
    pg = jax.nn.softmax(lg, axis=-1)
    _, gsel = lax.top_k(lg, 1)
    le = ((h2 @ re_w).astype(jnp.float32) + re_b.astype(jnp.float32)).reshape(t, N_GROUPS, EXPERTS_PER_GROUP)
    le_g = jnp.take_along_axis(le, gsel[:, :, None], axis=1)[:, 0]
    ev, ei = lax.top_k(le_g, TOP_K)
    wts = jnp.take_along_axis(pg, gsel, axis=1) * jax.nn.softmax(ev, axis=-1)
    eid = gsel * EXPERTS_PER_GROUP + ei
    combine = jnp.sum(jax.nn.one_hot(eid, N_EXPERTS, dtype=jnp.float32) * wts[..., None], axis=1)
    combine = combine.astype(h.dtype)
    out = jnp.zeros_like(h2)
    for e in range(N_EXPERTS):
        he = jax.nn.silu(h2 @ w_gate[e]) * (h2 @ w_up[e])
        out = out + combine[:, e:e + 1] * (he @ w_down[e])
    return out.reshape(shp)


def _layer(x, buf_a, buf_qkv, s0, segments, norm_mix, w_in, conv_a_w, conv_qkv_w, a_log, dt_bias,
           o_norm, w_out, norm_ffn, rg_w, rg_b, re_w, re_b, w_gate, w_up, w_down):
    b, length, _ = x.shape
    h = _rmsnorm(x, norm_mix)
    z = h @ w_in
    cuts = [int(c) for c in np.cumsum(IN_SIZES)[:-1]]
    c_b, c_c, c_h, _q, _k, _v, gate, a_raw, b_raw = jnp.split(z, cuts, axis=-1)
    ca, new_buf_a = _causal_dwconv(c_c * c_h, buf_a, conv_a_w)
    y_a = c_b * ca
    qkv_raw = z[..., 3 * CONV_CH:3 * CONV_CH + 3 * GDN_DIM]
    qkv, new_buf_qkv = _causal_dwconv(qkv_raw, buf_qkv, conv_qkv_w)
    qkv = jax.nn.silu(qkv).astype(jnp.float32)
    q, k, v = jnp.split(qkv, 3, axis=-1)
    q = _l2norm(q.reshape(b, length, GDN_HEADS, GDN_HEAD_DIM)) * (GDN_HEAD_DIM ** -0.5)
    k = _l2norm(k.reshape(b, length, GDN_HEADS, GDN_HEAD_DIM))
    v = v.reshape(b, length, GDN_HEADS, GDN_HEAD_DIM)
    g = -jnp.exp(a_log.astype(jnp.float32)) * jax.nn.softplus(
        a_raw.astype(jnp.float32) + dt_bias.astype(jnp.float32))
    beta = jax.nn.sigmoid(b_raw.astype(jnp.float32))
    s = s0.astype(jnp.float32)
    outs = []
    start = 0
    for seg_len, chunk in segments:
        sl = slice(start, start + seg_len)
        o_seg, s = _gdn_chunks(q[:, sl], k[:, sl], v[:, sl], g[:, sl], beta[:, sl], s, chunk)
        outs.append(o_seg)
        start += seg_len
    o = jnp.concatenate(outs, axis=1) if len(outs) > 1 else outs[0]
    o = _rmsnorm(o, o_norm) * jax.nn.silu(
        gate.astype(jnp.float32).reshape(b, length, GDN_HEADS, GDN_HEAD_DIM))
    y_b = o.reshape(b, length, GDN_DIM).astype(x.dtype)
    x = x + jnp.concatenate([y_a, y_b], axis=-1) @ w_out
    x = x + _moe(_rmsnorm(x, norm_ffn), rg_w, rg_b, re_w, re_b, w_gate, w_up, w_down)
    return x, new_buf_a, new_buf_qkv, s


def setup_inputs(seed: int = 0) -> dict:
    key = jax.random.key(seed)
    ks = jax.random.split(key, 24)
    f32 = jnp.float32
    nrm = lambda k, shp, sc: jax.random.normal(k, shp, f32) * sc
    dt = jnp.exp(jax.random.uniform(ks[9], (DEPTH, GDN_HEADS), f32, math.log(1e-3), math.log(1e-1)))
    return {
        "x_prompt": nrm(ks[0], (BATCH, SEQ, D_MODEL), 1.0),
        "x_sample": nrm(ks[1], (DEC_BATCH, DEC_SEQ, D_MODEL), 1.0),
        "state_conv_a": nrm(ks[2], (DEPTH, DEC_BATCH, CONV_A_K - 1, CONV_CH), 1.0),
        "state_conv_qkv": nrm(ks[3], (DEPTH, DEC_BATCH, CONV_QKV_K - 1, 3 * GDN_DIM), 1.0),
        "state_ssm": nrm(ks[4], (DEPTH, DEC_BATCH, GDN_HEADS, GDN_HEAD_DIM, GDN_HEAD_DIM), 0.1),
        "meta_tokens": nrm(ks[5], (N_META, D_MODEL), 1.0),
        "norm_mix": 1.0 + nrm(ks[6], (DEPTH, D_MODEL), 0.02),
        "w_in": nrm(ks[7], (DEPTH, D_MODEL, N_IN), D_MODEL ** -0.5),
        "conv_a_w": nrm(ks[8], (DEPTH, CONV_A_K, CONV_CH), CONV_A_K ** -0.5),
        "conv_qkv_w": nrm(ks[10], (DEPTH, CONV_QKV_K, 3 * GDN_DIM), CONV_QKV_K ** -0.5),
        "a_log": jnp.log(jax.random.uniform(ks[11], (DEPTH, GDN_HEADS), f32, 1.0, 16.0)),
        "dt_bias": dt + jnp.log(-jnp.expm1(-dt)),
        "o_norm": 1.0 + nrm(ks[12], (DEPTH, GDN_HEAD_DIM), 0.02),
        "w_out": nrm(ks[13], (DEPTH, D_MIX, D_MODEL), D_MIX ** -0.5),
        "norm_ffn": 1.0 + nrm(ks[14], (DEPTH, D_MODEL), 0.02),
        "router_group_w": nrm(ks[15], (DEPTH, D_MODEL, N_GROUPS), D_MODEL ** -0.5),
        "router_group_b": nrm(ks[16], (DEPTH, N_GROUPS), 0.01),
        "router_expert_w": nrm(ks[17], (DEPTH, D_MODEL, N_EXPERTS), D_MODEL ** -0.5),
        "router_expert_b": nrm(ks[18], (DEPTH, N_EXPERTS), 0.01),
        "w_gate": nrm(ks[19], (DEPTH, N_EXPERTS, D_MODEL, D_FF), D_MODEL ** -0.5),
        "w_up": nrm(ks[20], (DEPTH, N_EXPERTS, D_MODEL, D_FF), D_MODEL ** -0.5),
        "w_down": nrm(ks[21], (DEPTH, N_EXPERTS, D_FF, D_MODEL), D_FF ** -0.5),
        "norm_final": 1.0 + nrm(ks[22], (D_MODEL,), 0.02),
    }


def reference(x_prompt, x_sample, state_conv_a, state_conv_qkv, state_ssm, meta_tokens, norm_mix, w_in,
              conv_a_w, conv_qkv_w, a_log, dt_bias, o_norm, w_out, norm_ffn, router_group_w,
              router_group_b, router_expert_w, router_expert_b, w_gate, w_up, w_down, norm_final):
    dtype = x_prompt.dtype
    xp = jnp.concatenate([jnp.broadcast_to(meta_tokens.astype(dtype)[None], (BATCH, N_META, D_MODEL)),
                          x_prompt], axis=1)
    xs = x_sample
    prompt_segments = ((N_META, N_META), (SEQ, CHUNK))
    sample_segments = ((DEC_SEQ, DEC_SEQ),)
    pa, pq, ps, sa, sq, ss = [], [], [], [], [], []
    for l in range(DEPTH):
        lw = (norm_mix[l], w_in[l], conv_a_w[l], conv_qkv_w[l], a_log[l], dt_bias[l], o_norm[l], w_out[l],
              norm_ffn[l], router_group_w[l], router_group_b[l], router_expert_w[l], router_expert_b[l],
              w_gate[l], w_up[l], w_down[l])
        xp, ba, bq, s = _layer(xp,
                               jnp.zeros((BATCH, CONV_A_K - 1, CONV_CH), dtype),
                               jnp.zeros((BATCH, CONV_QKV_K - 1, 3 * GDN_DIM), dtype),
                               jnp.zeros((BATCH, GDN_HEADS, GDN_HEAD_DIM, GDN_HEAD_DIM), jnp.float32),
                               prompt_segments, *lw)
        pa.append(ba.astype(state_conv_a.dtype))
        pq.append(bq.astype(state_conv_qkv.dtype))
        ps.append(s.astype(state_ssm.dtype))
        xs, ba, bq, s = _layer(xs, state_conv_a[l], state_conv_qkv[l], state_ssm[l], sample_segments, *lw)
        sa.append(ba.astype(state_conv_a.dtype))
        sq.append(bq.astype(state_conv_qkv.dtype))
        ss.append(s.astype(state_ssm.dtype))
    y_prompt = _rmsnorm(xp, norm_final)[:, N_META:]
    y_sample = _rmsnorm(xs, norm_final)
    return (y_prompt, y_sample, jnp.stack(pa), jnp.stack(pq), jnp.stack(ps),
            jnp.stack(sa), jnp.stack(sq), jnp.stack(ss))
```

```python
import functools
import math

import jax
import jax.numpy as jnp
from jax import lax
from jax.experimental import pallas as pl
from jax.experimental.pallas import tpu as pltpu

F32 = jnp.float32
BF16 = jnp.bfloat16

D_MODEL = 1024
CONV_CH = 512
N_HEADS = 4
HEAD_DIM = 128
GDN_DIM = N_HEADS * HEAD_DIM
N_GROUPS = 4
EXPERTS_PER_GROUP = 8
N_EXPERTS = N_GROUPS * EXPERTS_PER_GROUP
D_FF = 256
EPS = 1e-6
N_MAIN = 3 * CONV_CH + 4 * GDN_DIM
LANES = 128
SUBLANES = 8

MIX_BLOCK = 256
MIX_CHUNK = 64
DEC_GROUP = 8
MOE_TILE = 384
VMEM_LIMIT = 56 * 1024 * 1024


def _dot(a, b):
    return jnp.dot(a, b, preferred_element_type=F32)


def _dot_nt(a, b):
    return lax.dot_general(a, b, (((1,), (1,)), ((), ())), preferred_element_type=F32)


def _dot_tn(a, b):
    return lax.dot_general(a, b, (((0,), (0,)), ((), ())), preferred_element_type=F32)


def _split_bf16(x):
    hi = x.astype(BF16)
    lo = (x - hi.astype(F32)).astype(BF16)
    return hi, lo


def _dot3(a, b):
    a_hi, a_lo = _split_bf16(a)
    b_hi, b_lo = _split_bf16(b)
    return _dot(a_hi, b_hi) + (_dot(a_hi, b_lo) + _dot(a_lo, b_hi))


def _rmsnorm(x, w):
    return x * lax.rsqrt(jnp.mean(x * x, axis=-1, keepdims=True) + EPS) * w


def _silu(x):
    return x / (1.0 + jnp.exp(-x))


def _sigmoid(x):
    return 1.0 / (1.0 + jnp.exp(-x))


def _softplus(x):
    return jnp.maximum(x, 0.0) + jnp.log1p(jnp.exp(-jnp.abs(x)))


def _head_norm(x, scale):
    outs = []
    for h in range(N_HEADS):
        xh = x[:, h * HEAD_DIM:(h + 1) * HEAD_DIM]
        outs.append(xh * (lax.rsqrt(jnp.sum(xh * xh, axis=-1, keepdims=True) + EPS) * scale))
    return jnp.concatenate(outs, axis=1)


def _gate_decay(ab, a_log, dt_bias):
    g = -jnp.exp(a_log) * _softplus(ab + dt_bias)
    lane = lax.broadcasted_iota(jnp.int32, ab.shape, 1)
    g = jnp.where(lane < N_HEADS, g, 0.0)
    return g, _sigmoid(ab)


def _out_gate(o, gate, o_norm):
    outs = []
    for h in range(N_HEADS):
        hs = slice(h * HEAD_DIM, (h + 1) * HEAD_DIM)
        outs.append(_rmsnorm(o[:, hs], o_norm) * _silu(gate[:, hs]))
    return jnp.concatenate(outs, axis=1)


def _mixer_block(x, L, C, w, ubuf, qbuf, s_ref, with_out):
    (norm_mix, w_main, w_ab, conv_a_w, conv_q_w, a_log, dt_bias, o_norm, w_out) = w
    h = _rmsnorm(x, norm_mix[...]).astype(BF16)
    z = _dot(h, w_main[...])
    ab = _dot(h, w_ab[...])

    c_b = z[:, 0:CONV_CH]
    ubuf[pl.ds(SUBLANES, L), :] = z[:, CONV_CH:2 * CONV_CH] * z[:, 2 * CONV_CH:3 * CONV_CH]
    ca = (ubuf[pl.ds(SUBLANES - 2, L), :] * conv_a_w[0:1, :]
          + ubuf[pl.ds(SUBLANES - 1, L), :] * conv_a_w[1:2, :]
          + ubuf[pl.ds(SUBLANES, L), :] * conv_a_w[2:3, :])
    y_a = c_b * ca

    q0 = 3 * CONV_CH
    qbuf[pl.ds(SUBLANES, L), :] = z[:, q0:q0 + 3 * GDN_DIM]
    qkv = (qbuf[pl.ds(SUBLANES - 3, L), :] * conv_q_w[0:1, :]
           + qbuf[pl.ds(SUBLANES - 2, L), :] * conv_q_w[1:2, :]
           + qbuf[pl.ds(SUBLANES - 1, L), :] * conv_q_w[2:3, :]
           + qbuf[pl.ds(SUBLANES, L), :] * conv_q_w[3:4, :])
    qkv = _silu(qkv)
    qn = _head_norm(qkv[:, 0:GDN_DIM], HEAD_DIM ** -0.5)
    kn = _head_norm(qkv[:, GDN_DIM:2 * GDN_DIM], 1.0)
    v = qkv[:, 2 * GDN_DIM:3 * GDN_DIM]
    gate = z[:, q0 + 3 * GDN_DIM:q0 + 4 * GDN_DIM]

    g, beta = _gate_decay(ab, a_log[...], dt_bias[...])

    r = lax.broadcasted_iota(jnp.int32, (L, L), 0)
    cidx = lax.broadcasted_iota(jnp.int32, (L, L), 1)
    sh = int(math.log2(C))
    tri = jnp.where((cidx <= r) & ((r >> sh) == (cidx >> sh)), 1.0, 0.0).astype(BF16)
    g_hi, g_lo = _split_bf16(g)
    gc = _dot(tri, g_hi) + _dot(tri, g_lo)
    if L % LANES:
        gct = jnp.concatenate([gc, jnp.zeros((LANES - L % LANES, LANES), F32)], axis=0).T
    else:
        gct = gc.T

    rc = lax.broadcasted_iota(jnp.int32, (C, C), 0)
    cc = lax.broadcasted_iota(jnp.int32, (C, C), 1)
    causal = rc >= cc
    strict = rc > cc
    eye = jnp.where(rc == cc, 1.0, 0.0).astype(F32)
    n_sq = max(int(math.log2(C)) - 1, 0)

    o_rows = []
    for c in range(L // C):
        sl = slice(c * C, (c + 1) * C)
        o_heads = []
        for hd in range(N_HEADS):
            hs = slice(hd * HEAD_DIM, (hd + 1) * HEAD_DIM)
            qh, kh, vh = qn[sl, hs], kn[sl, hs], v[sl, hs]
            gcol = gc[sl, hd:hd + 1]
            grow = gct[hd:hd + 1, sl]
            bcol = beta[sl, N_HEADS + hd:N_HEADS + hd + 1]
            ecol = jnp.exp(gcol)
            decay = jnp.exp(jnp.where(causal, gcol - grow, -jnp.inf))
            kb = kh * bcol
            kk_qk = _dot_nt(jnp.concatenate([kb, qh], axis=0).astype(BF16), kh.astype(BF16))
            lmat = jnp.where(strict, kk_qk[0:C] * decay, 0.0)
            attn = kk_qk[C:2 * C] * decay
            p = -lmat
            t = eye + p
            for _ in range(n_sq):
                p = _dot3(p, p)
                t = t + _dot3(t, p)
            rhs = jnp.concatenate([vh * bcol, kb * ecol], axis=1)
            sol = _dot3(t, rhs)
            s_old = s_ref[hd]
            r2 = _dot(jnp.concatenate([sol[:, HEAD_DIM:], qh * ecol], axis=0).astype(BF16),
                      s_old.astype(BF16))
            u = sol[:, 0:HEAD_DIM] - r2[0:C]
            ub = u.astype(BF16)
            glast = gcol[C - 1:C, :]
            o_heads.append(r2[C:2 * C] + _dot(attn.astype(BF16), ub))
            kdec = (kh * jnp.exp(glast - gcol)).astype(BF16)
            s_ref[hd] = s_old * jnp.exp(glast) + _dot_tn(kdec, ub)
        o_rows.append(jnp.concatenate(o_heads, axis=1))
    o = o_rows[0] if len(o_rows) == 1 else jnp.concatenate(o_rows, axis=0)
    if not with_out:
        return None
    y_b = _out_gate(o, gate, o_norm[...])
    y = jnp.concatenate([y_a, y_b], axis=1).astype(BF16)
    return x + _dot(y, w_out[...])


def _meta_kernel(x_ref, norm_mix, w_main, w_ab, conv_a_w, conv_q_w, a_log, dt_bias, o_norm, w_out,
                 ua_ref, qc_ref, s_out, ubuf, qbuf, s_ref):
    L = x_ref.shape[0]
    ubuf[...] = jnp.zeros_like(ubuf)
    qbuf[...] = jnp.zeros_like(qbuf)
    s_ref[...] = jnp.zeros_like(s_ref)
    w = (norm_mix, w_main, w_ab, conv_a_w, conv_q_w, a_log, dt_bias, o_norm, w_out)
    _mixer_block(x_ref[...], L, L, w, ubuf, qbuf, s_ref, with_out=False)
    ua_ref[...] = ubuf[pl.ds(L, SUBLANES), :]
    qc_ref[...] = qbuf[pl.ds(L, SUBLANES), :]
    s_out[...] = s_ref[...]


def _prompt_mixer_kernel(x_ref, ua0, qc0, s0, norm_mix, w_main, w_ab, conv_a_w, conv_q_w, a_log,
                         dt_bias, o_norm, w_out, x1_ref, ca_out, cq_out, s_out, ubuf, qbuf, s_ref):
    j = pl.program_id(1)
    L = x_ref.shape[0]

    @pl.when(j == 0)
    def _():
        ubuf[pl.ds(0, SUBLANES), :] = ua0[...]
        qbuf[pl.ds(0, SUBLANES), :] = qc0[...]
        s_ref[...] = s0[...]

    w = (norm_mix, w_main, w_ab, conv_a_w, conv_q_w, a_log, dt_bias, o_norm, w_out)
    x1_ref[...] = _mixer_block(x_ref[...], L, MIX_CHUNK, w, ubuf, qbuf, s_ref, with_out=True)
    ubuf[pl.ds(0, SUBLANES), :] = ubuf[pl.ds(L, SUBLANES), :]
    qbuf[pl.ds(0, SUBLANES), :] = qbuf[pl.ds(L, SUBLANES), :]

    @pl.when(j == pl.num_programs(1) - 1)
    def _():
        ca_out[...] = ubuf[pl.ds(SUBLANES - 2, 2), :]
        cq_out[...] = qbuf[pl.ds(SUBLANES - 3, 3), :]
        s_out[...] = s_ref[...]


def _decode_mixer_kernel(x_ref, ca_in, cq_in, s_in, norm_mix, w_main, w_ab, conv_a_w, conv_q_w, a_log,
                         dt_bias, o_norm, w_out, xbuf_in, x1_ref, ca_out, cq_out, s_out,
                         a1_s, a2_s, k_s, vb_s, qk_s, eg_s, o_s, ya_s, gate_s):
    del xbuf_in
    i = pl.program_id(0)
    nb = x_ref.shape[0]

    @pl.when(i == 0)
    def _():
        x = x_ref[...]
        h = _rmsnorm(x, norm_mix[...]).astype(BF16)
        z = _dot(h, w_main[...])
        ab = _dot(h, w_ab[...])
        u = z[:, CONV_CH:2 * CONV_CH] * z[:, 2 * CONV_CH:3 * CONV_CH]
        b0, b1 = ca_in[:, 0:CONV_CH], ca_in[:, CONV_CH:2 * CONV_CH]
        ca = b0 * conv_a_w[0:1, :] + b1 * conv_a_w[1:2, :] + u * conv_a_w[2:3, :]
        ya_s[...] = z[:, 0:CONV_CH] * ca
        ca_out[:, 0:CONV_CH] = b1
        ca_out[:, CONV_CH:2 * CONV_CH] = u
        q0 = 3 * CONV_CH
        W = 3 * GDN_DIM
        raw = z[:, q0:q0 + W]
        p0, p1, p2 = cq_in[:, 0:W], cq_in[:, W:2 * W], cq_in[:, 2 * W:3 * W]
        qkv = (p0 * conv_q_w[0:1, :] + p1 * conv_q_w[1:2, :] + p2 * conv_q_w[2:3, :]
               + raw * conv_q_w[3:4, :])
        cq_out[:, 0:W] = p1
        cq_out[:, W:2 * W] = p2
        cq_out[:, 2 * W:3 * W] = raw
        qkv = _silu(qkv)
        qn = _head_norm(qkv[:, 0:GDN_DIM], HEAD_DIM ** -0.5)
        kn = _head_norm(qkv[:, GDN_DIM:2 * GDN_DIM], 1.0)
        v = qkv[:, 2 * GDN_DIM:3 * GDN_DIM]
        gate_s[...] = z[:, q0 + W:q0 + W + GDN_DIM]
        g, beta = _gate_decay(ab, a_log[...], dt_bias[...])
        eg = jnp.exp(g)
        eg_s[...] = eg
        qk_cols = []
        for hd in range(N_HEADS):
            hs = slice(hd * HEAD_DIM, (hd + 1) * HEAD_DIM)
            e_h = eg[:, hd:hd + 1]
            b_h = beta[:, N_HEADS + hd:N_HEADS + hd + 1]
            a1_s[:, hs] = kn[:, hs] * (b_h * e_h)
            a2_s[:, hs] = qn[:, hs] * e_h
            vb_s[:, hs] = v[:, hs] * b_h
            qk_cols.append(jnp.sum(qn[:, hs] * kn[:, hs], axis=-1, keepdims=True))
        k_s[...] = kn
        qk_s[...] = jnp.concatenate(qk_cols + [jnp.zeros((nb, LANES - N_HEADS), F32)], axis=1)

    r0 = pl.multiple_of(i * DEC_GROUP, DEC_GROUP)
    rows = pl.ds(r0, DEC_GROUP)
    rid = lax.broadcasted_iota(jnp.int32, (DEC_GROUP, HEAD_DIM), 0)
    eg_g = eg_s[rows, :]
    qk_g = qk_s[rows, :]
    for hd in range(N_HEADS):
        hs = slice(hd * HEAD_DIM, (hd + 1) * HEAD_DIM)
        lhs = jnp.concatenate([a1_s[rows, hs], a2_s[rows, hs]], axis=0).astype(BF16)
        k_g = k_s[rows, hs]
        vb_g = vb_s[rows, hs]
        o_g = jnp.zeros((DEC_GROUP, HEAD_DIM), F32)
        for b in range(DEC_GROUP):
            s_old = s_in[b, hd]
            res = _dot(lhs, s_old.astype(BF16))
            u_all = vb_g - res[0:DEC_GROUP]
            o_all = res[DEC_GROUP:2 * DEC_GROUP] + qk_g[:, hd:hd + 1] * u_all
            o_g = jnp.where(rid == b, o_all, o_g)
            k_only = jnp.where(rid == b, k_g, 0.0).astype(BF16)
            s_out[b, hd] = s_old * eg_g[b:b + 1, hd:hd + 1] + _dot_tn(k_only, u_all.astype(BF16))
        o_s[rows, hs] = o_g

    @pl.when(i == pl.num_programs(0) - 1)
    def _():
        y_b = _out_gate(o_s[...], gate_s[...], o_norm[...])
        y = jnp.concatenate([ya_s[...], y_b], axis=1).astype(BF16)
        x1_ref[...] = x_ref[...] + _dot(y, w_out[...])


def _route(h2, rw_hi, rw_lo, rb):
    h_hi, h_lo = _split_bf16(h2)
    logits = _dot(h_hi, rw_hi) + _dot(h_lo, rw_hi) + _dot(h_hi, rw_lo) + rb
    t = h2.shape[0]
    lane = lax.broadcasted_iota(jnp.int32, (t, LANES), 1)
    neg = -jnp.inf
    big = jnp.int32(LANES)
    lg = jnp.where(lane < N_GROUPS, logits, neg)
    gmax = jnp.max(lg, axis=-1, keepdims=True)
    gsel = jnp.min(jnp.where(lg == gmax, lane, big), axis=-1, keepdims=True)
    pg = 1.0 / jnp.sum(jnp.exp(lg - gmax), axis=-1, keepdims=True)
    eidx = lane - N_GROUPS
    in_group = (lane >= N_GROUPS) & (lane < N_GROUPS + N_EXPERTS) & ((eidx >> 3) == gsel)
    le = jnp.where(in_group, logits, neg)
    m1 = jnp.max(le, axis=-1, keepdims=True)
    i1 = jnp.min(jnp.where(le == m1, lane, big), axis=-1, keepdims=True)
    le2 = jnp.where(lane == i1, neg, le)
    m2 = jnp.max(le2, axis=-1, keepdims=True)
    i2 = jnp.min(jnp.where(le2 == m2, lane, big), axis=-1, keepdims=True)
    e2 = jnp.exp(m2 - m1)
    p1 = pg / (1.0 + e2)
    p2 = pg * e2 / (1.0 + e2)
    comb = jnp.where(lane == i1, p1, 0.0) + jnp.where(lane == i2, p2, 0.0)
    return comb


def _moe_dense_kernel(x_ref, norm_ffn, rw_hi, rw_lo, rb, wg, wu, wd, norm_final, y_ref,
                      h_s, comb_s, acc_s):
    e = pl.program_id(1)

    @pl.when(e == 0)
    def _():
        h2 = _rmsnorm(x_ref[...], norm_ffn[...])
        h_s[...] = h2.astype(BF16)
        comb_s[...] = _route(h2, rw_hi[...], rw_lo[...], rb[...])
        acc_s[...] = jnp.zeros_like(acc_s)

    hb = h_s[...]
    he = _silu(_dot(hb, wg[...])) * _dot(hb, wu[...])
    lane = lax.broadcasted_iota(jnp.int32, comb_s.shape, 1)
    c_e = jnp.sum(jnp.where(lane == e + N_GROUPS, comb_s[...], 0.0), axis=-1, keepdims=True)
    acc_s[...] += c_e * _dot(he.astype(BF16), wd[...])

    @pl.when(e == pl.num_programs(1) - 1)
    def _():
        y_ref[...] = _rmsnorm(x_ref[...] + acc_s[...], norm_final[...])


def _full(shape):
    n = len(shape)
    return pl.BlockSpec(shape, lambda *_: (0,) * n)


def _pad_lanes(row):
    return jnp.zeros((1, LANES), F32).at[0, :row.shape[0]].set(row.astype(F32))


def kernel(x_prompt, x_sample, state_conv_a, state_conv_qkv, state_ssm, meta_tokens, norm_mix, w_in,
           conv_a_w, conv_qkv_w, a_log, dt_bias, o_norm, w_out, norm_ffn, router_group_w,
           router_group_b, router_expert_w, router_expert_b, w_gate, w_up, w_down, norm_final):
    B, S, D = x_prompt.shape
    NB = x_sample.shape[0]
    n_meta = meta_tokens.shape[0]
    assert D == D_MODEL and S % MIX_BLOCK == 0 and NB % DEC_GROUP == 0 and n_meta % SUBLANES == 0
    assert w_in.shape[0] == 1, "single-layer trunk"

    w_main = w_in[0, :, :N_MAIN].astype(BF16)
    w_ab = jnp.zeros((D, LANES), F32).at[:, :2 * N_HEADS].set(w_in[0, :, N_MAIN:]).astype(BF16)
    w_o = w_out[0].astype(BF16)
    nm = norm_mix[0].reshape(1, D)
    nf = norm_ffn[0].reshape(1, D)
    nfin = norm_final.reshape(1, D)
    onorm = o_norm[0].reshape(1, HEAD_DIM)
    alog = _pad_lanes(a_log[0])
    dtb = _pad_lanes(dt_bias[0])
    caw = conv_a_w[0]
    cqw = conv_qkv_w[0]
    r_w = jnp.zeros((D, LANES), F32).at[:, :N_GROUPS].set(router_group_w[0])
    r_w = r_w.at[:, N_GROUPS:N_GROUPS + N_EXPERTS].set(router_expert_w[0])
    rw_hi, rw_lo = _split_bf16(r_w)
    r_b = _pad_lanes(jnp.concatenate([router_group_b[0], router_expert_b[0]]))
    wg = w_gate[0].astype(BF16)
    wu = w_up[0].astype(BF16)
    wd = w_down[0].astype(BF16)

    layer_w = (nm, w_main, w_ab, caw, cqw, alog, dtb, onorm, w_o)
    layer_specs = [_full(a.shape) for a in layer_w]
    cparams = functools.partial(pltpu.CompilerParams, vmem_limit_bytes=VMEM_LIMIT)

    ua0, qc0, s_meta = pl.pallas_call(
        _meta_kernel,
        out_shape=(jax.ShapeDtypeStruct((SUBLANES, CONV_CH), F32),
                   jax.ShapeDtypeStruct((SUBLANES, 3 * GDN_DIM), F32),
                   jax.ShapeDtypeStruct((N_HEADS, HEAD_DIM, HEAD_DIM), F32)),
        grid=(1,),
        in_specs=[_full(meta_tokens.shape)] + layer_specs,
        out_specs=(_full((SUBLANES, CONV_CH)), _full((SUBLANES, 3 * GDN_DIM)),
                   _full((N_HEADS, HEAD_DIM, HEAD_DIM))),
        scratch_shapes=[pltpu.VMEM((SUBLANES + n_meta, CONV_CH), F32),
                        pltpu.VMEM((SUBLANES + n_meta, 3 * GDN_DIM), F32),
                        pltpu.VMEM((N_HEADS, HEAD_DIM, HEAD_DIM), F32)],
        compiler_params=cparams(dimension_semantics=("arbitrary",)),
        name="meta_mixer",
    )(meta_tokens.astype(F32), *layer_w)

    n_tok = B * S + NB
    nj = S // MIX_BLOCK
    x1_all, pa, pq, ps = pl.pallas_call(
        _prompt_mixer_kernel,
        out_shape=(jax.ShapeDtypeStruct((n_tok, D), F32),
                   jax.ShapeDtypeStruct((B, 2, CONV_CH), F32),
                   jax.ShapeDtypeStruct((B, 3, 3 * GDN_DIM), F32),
                   jax.ShapeDtypeStruct((B, N_HEADS, HEAD_DIM, HEAD_DIM), F32)),
        grid=(B, nj),
        in_specs=[pl.BlockSpec((None, MIX_BLOCK, D), lambda b, j: (b, j, 0)),
                  _full(ua0.shape), _full(qc0.shape), _full(s_meta.shape)] + layer_specs,
        out_specs=(pl.BlockSpec((MIX_BLOCK, D), lambda b, j: (b * nj + j, 0)),
                   pl.BlockSpec((None, 2, CONV_CH), lambda b, j: (b, 0, 0)),
                   pl.BlockSpec((None, 3, 3 * GDN_DIM), lambda b, j: (b, 0, 0)),
                   pl.BlockSpec((None, N_HEADS, HEAD_DIM, HEAD_DIM), lambda b, j: (b, 0, 0, 0))),
        scratch_shapes=[pltpu.VMEM((SUBLANES + MIX_BLOCK, CONV_CH), F32),
                        pltpu.VMEM((SUBLANES + MIX_BLOCK, 3 * GDN_DIM), F32),
                        pltpu.VMEM((N_HEADS, HEAD_DIM, HEAD_DIM), F32)],
        compiler_params=cparams(dimension_semantics=("arbitrary", "arbitrary")),
        name="prompt_mixer",
    )(x_prompt, ua0, qc0, s_meta, *layer_w)

    assert (B * S) % NB == 0
    xs2 = x_sample.reshape(NB, D)
    ca2 = state_conv_a[0].reshape(NB, 2 * CONV_CH)
    cq2 = state_conv_qkv[0].reshape(NB, 9 * GDN_DIM)
    dec_scratch = ([pltpu.VMEM((NB, GDN_DIM), F32)] * 4
                   + [pltpu.VMEM((NB, LANES), F32)] * 2
                   + [pltpu.VMEM((NB, GDN_DIM), F32)] * 3)
    x1_all, sa, sq, ss = pl.pallas_call(
        _decode_mixer_kernel,
        out_shape=(jax.ShapeDtypeStruct((n_tok, D), F32),
                   jax.ShapeDtypeStruct((NB, 2 * CONV_CH), F32),
                   jax.ShapeDtypeStruct((NB, 9 * GDN_DIM), F32),
                   jax.ShapeDtypeStruct((NB, N_HEADS, HEAD_DIM, HEAD_DIM), F32)),
        grid=(NB // DEC_GROUP,),
        in_specs=[_full(xs2.shape), _full(ca2.shape), _full(cq2.shape),
                  pl.BlockSpec((DEC_GROUP, N_HEADS, HEAD_DIM, HEAD_DIM), lambda i: (i, 0, 0, 0))]
                 + layer_specs + [pl.BlockSpec(memory_space=pl.ANY)],
        out_specs=(pl.BlockSpec((NB, D), lambda i: ((B * S) // NB, 0)),
                   _full(ca2.shape), _full(cq2.shape),
                   pl.BlockSpec((DEC_GROUP, N_HEADS, HEAD_DIM, HEAD_DIM), lambda i: (i, 0, 0, 0))),
        scratch_shapes=dec_scratch,
        input_output_aliases={4 + len(layer_w): 0},
        compiler_params=cparams(dimension_semantics=("arbitrary",)),
        name="decode_mixer",
    )(xs2, ca2, cq2, state_ssm[0], *layer_w, x1_all)

    assert n_tok % MOE_TILE == 0
    y_all = pl.pallas_call(
        _moe_dense_kernel,
        out_shape=jax.ShapeDtypeStruct((n_tok, D), F32),
        grid=(n_tok // MOE_TILE, N_EXPERTS),
        in_specs=[pl.BlockSpec((MOE_TILE, D), lambda t, e: (t, 0)),
                  _full(nf.shape), _full(rw_hi.shape), _full(rw_lo.shape), _full(r_b.shape),
                  pl.BlockSpec((None, D, D_FF), lambda t, e: (e, 0, 0)),
                  pl.BlockSpec((None, D, D_FF), lambda t, e: (e, 0, 0)),
                  pl.BlockSpec((None, D_FF, D), lambda t, e: (e, 0, 0)),
                  _full(nfin.shape)],
        out_specs=pl.BlockSpec((MOE_TILE, D), lambda t, e: (t, 0)),
        scratch_shapes=[pltpu.VMEM((MOE_TILE, D), BF16),
                        pltpu.VMEM((MOE_TILE, LANES), F32),
                        pltpu.VMEM((MOE_TILE, D), F32)],
        compiler_params=cparams(dimension_semantics=("arbitrary", "arbitrary")),
        name="moe_dense",
    )(x1_all, nf, rw_hi, rw_lo, r_b, wg, wu, wd, nfin)

    y_prompt = y_all[:B * S].reshape(B, S, D)
    y_sample = y_all[B * S:].reshape(NB, 1, D)
    return (y_prompt, y_sample, pa[None], pq[None], ps[None],
            sa.reshape(1, NB, 2, CONV_CH), sq.reshape(1, NB, 3, 3 * GDN_DIM), ss[None])
```
